```python
import math
import jax
import jax.numpy as jnp
from jax import lax
import numpy as np

D_MODEL = 2048
BATCH = 4
SEQ = 4096
DEPTH = 2

GRID_W = 64
CTX_LEN = 256
N_MIXERS = 4
MIX_WIDTH = D_MODEL
GROUP_WIDTH = MIX_WIDTH // N_MIXERS
HEAD_DIM = 64
N_MOD = 6

NA_HEADS = GROUP_WIDTH // HEAD_DIM
NA_WIN_ROWS = 8
NA_WIN_COLS = 16
NA_QCOL_BLOCK = 16
NA_KCOL_SPAN = NA_QCOL_BLOCK + NA_WIN_COLS

WA_HEADS = GROUP_WIDTH // HEAD_DIM
WA_KV_HEADS = 2
WA_WINDOW = 128
WA_BLOCK = 128
ROPE_BASE = 10000.0

SSM_HEAD_DIM = 64
SSM_HEADS = GROUP_WIDTH // SSM_HEAD_DIM
SSM_GROUPS = 2
SSM_STATE = 128
SSM_CONV = 4
SSM_CHUNK = 128
SSM_CONV_CH = GROUP_WIDTH + 2 * SSM_GROUPS * SSM_STATE

GM_GROUPS = 8
GM_CHUNK = 128

N_EXPERTS = 32
TOP_K = 4
EXPERT_FF = D_MODEL // 2
SWIGLU_LIMIT = 7.0
SWIGLU_ALPHA = 1.702

NORM_EPS = 1e-6
NEG_INF = -1e30

IN_SPLITS = (GROUP_WIDTH, GROUP_WIDTH, GROUP_WIDTH,
             GROUP_WIDTH, WA_KV_HEADS * HEAD_DIM, WA_KV_HEADS * HEAD_DIM,
             GROUP_WIDTH, SSM_CONV_CH, SSM_HEADS,
             2 * GROUP_WIDTH)
IN_COLS = sum(IN_SPLITS)

kernel_name = "hybrid_natten_swa_ssd_gmlp_moe_dit"

F32 = jnp.float32


def rmsnorm(x, g):
    xf = x.astype(F32)
    y = xf * lax.rsqrt(jnp.mean(xf * xf, axis=-1, keepdims=True) + NORM_EPS)
    return (y * g.astype(F32)).astype(x.dtype)


def layernorm(x, g, b):
    xf = x.astype(F32)
    mu = jnp.mean(xf, axis=-1, keepdims=True)
    var = jnp.mean(jnp.square(xf - mu), axis=-1, keepdims=True)
    y = (xf - mu) * lax.rsqrt(var + NORM_EPS)
    return (y * g.astype(F32) + b.astype(F32)).astype(x.dtype)


def split_cols(p):
    idx = np.cumsum(np.array(IN_SPLITS))[:-1].tolist()
    return jnp.split(p, idx, axis=-1)


def heads(t, n):
    return t.reshape(t.shape[0], t.shape[1], n, HEAD_DIM)


def axial_rope(x):
    s = x.shape[1]
    t = jnp.arange(s)
    rows = (t // GRID_W).astype(F32)
    cols = (t % GRID_W).astype(F32)
    q = HEAD_DIM // 4
    inv = ROPE_BASE ** (-jnp.arange(q, dtype=F32) / q)
    xf = x.astype(F32)

    def rot(xh, pos):
        ang = pos[:, None] * inv[None, :]
        cos = jnp.cos(ang)[:, None, :]
        sin = jnp.sin(ang)[:, None, :]
        x1, x2 = xh[..., :q], xh[..., q:]
        return jnp.concatenate([x1 * cos - x2 * sin, x2 * cos + x1 * sin], axis=-1)

    half = HEAD_DIM // 2
    return jnp.concatenate([rot(xf[..., :half], rows), rot(xf[..., half:], cols)], axis=-1).astype(x.dtype)


def ctx_attention(qc, kc, vc, sink):
    b, lc, hq, d = qc.shape
    hk = kc.shape[2]
    g = hq // hk
    qg = qc.reshape(b, lc, hk, g, d)
    s = jnp.einsum("bqkgd,bskd->bkgqs", qg, kc).astype(F32) * (d ** -0.5)
    if sink is not None:
        col = jnp.broadcast_to(sink.astype(F32).reshape(hk, g)[None, :, :, None, None], s.shape[:-1] + (1,))
        s = jnp.concatenate([s, col], axis=-1)
    p = jax.nn.softmax(s, axis=-1)[..., :lc].astype(vc.dtype)
    return jnp.einsum("bkgqs,bskd->bqkgd", p, vc).reshape(b, lc, hq * d)


def neighborhood_attention(q, k, v, qc, kc, vc, rpb, ctx_out):
    b, s, h, d = q.shape
    rows = s // GRID_W
    kh = min(NA_WIN_ROWS, rows)
    ncb = GRID_W // NA_QCOL_BLOCK
    n_loc = kh * NA_KCOL_SPAN
    scale = d ** -0.5
    r = jnp.arange(rows)
    key_rows = jnp.clip(r - kh // 2, 0, rows - kh)[:, None] + jnp.arange(kh)
    j = jnp.arange(ncb)
    span_start = jnp.clip(j * NA_QCOL_BLOCK - NA_WIN_COLS // 2, 0, GRID_W - NA_KCOL_SPAN)
    key_cols = span_start[:, None] + jnp.arange(NA_KCOL_SPAN)
    qcol = j[:, None] * NA_QCOL_BLOCK + jnp.arange(NA_QCOL_BLOCK)
    win_start = jnp.clip(qcol - NA_WIN_COLS // 2, 0, GRID_W - NA_WIN_COLS)
    kc_b = key_cols[:, None, :]
    col_mask = (kc_b >= win_start[..., None]) & (kc_b < win_start[..., None] + NA_WIN_COLS)
    dr = key_rows - r[:, None] + (NA_WIN_ROWS - 1)
    dc = jnp.clip(kc_b - qcol[..., None] + NA_WIN_COLS - 1, 0, 2 * NA_WIN_COLS - 2)
    bias = rpb.astype(F32)[:, dr[:, None, None, :, None], dc[None, :, :, None, :]]
    bias = jnp.where(col_mask[None, None, :, :, None, :], bias, NEG_INF)
    bias = bias.transpose(1, 2, 0, 3, 4, 5).reshape(rows, ncb, h, NA_QCOL_BLOCK, n_loc)

    qg = q.reshape(b, rows, ncb, NA_QCOL_BLOCK, h, d)
    gr = key_rows[:, None, :, None]
    gc = key_cols[None, :, None, :]
    kg = k.reshape(b, rows, GRID_W, h, d)[:, gr, gc].reshape(b, rows, ncb, n_loc, h, d)
    vg = v.reshape(b, rows, GRID_W, h, d)[:, gr, gc].reshape(b, rows, ncb, n_loc, h, d)
    s_loc = jnp.einsum("brnqhd,brnkhd->brnhqk", qg, kg).astype(F32) * scale + bias[None]
    s_ctx = jnp.einsum("brnqhd,bchd->brnhqc", qg, kc).astype(F32) * scale
    p = jax.nn.softmax(jnp.concatenate([s_loc, s_ctx], axis=-1), axis=-1).astype(v.dtype)
    o = (jnp.einsum("brnhqk,brnkhd->brnqhd", p[..., :n_loc], vg)
         + jnp.einsum("brnhqc,bchd->brnqhd", p[..., n_loc:], vc))
    o = o.reshape(b, s, h * d)
    oc = ctx_attention(qc, kc, vc, None) if ctx_out else None
    return o, oc


def window_gqa(q, k, v, qc, kc, vc, sink, ctx_out):
    b, s, hq, d = q.shape
    hk = k.shape[2]
    g = hq // hk
    nb = s // WA_BLOCK
    n_loc = 3 * WA_BLOCK
    lc = kc.shape[1]
    scale = d ** -0.5
    pad = ((0, 0), (WA_BLOCK, WA_BLOCK), (0, 0), (0, 0))

    def band(t):
        tb = jnp.pad(t, pad).reshape(b, nb + 2, WA_BLOCK, hk, d)
        return jnp.concatenate([tb[:, :-2], tb[:, 1:-1], tb[:, 2:]], axis=2)

    kb, vb = band(k), band(v)
    qb = q.reshape(b, nb, WA_BLOCK, hk, g, d)
    qi = jnp.arange(WA_BLOCK)
    kj = jnp.arange(n_loc)
    kpos = (jnp.arange(nb)[:, None] - 1) * WA_BLOCK + kj[None, :]
    in_band = jnp.abs(WA_BLOCK + qi[:, None] - kj[None, :]) <= WA_WINDOW
    mask = in_band[None] & ((kpos >= 0) & (kpos < s))[:, None, :]
    s_loc = jnp.einsum("bnqkgd,bnjkd->bnkgqj", qb, kb).astype(F32) * scale
    s_loc = jnp.where(mask[None, :, None, None], s_loc, NEG_INF)
    s_ctx = jnp.einsum("bnqkgd,bckd->bnkgqc", qb, kc).astype(F32) * scale
    sink_col = jnp.broadcast_to(sink.astype(F32).reshape(hk, g)[None, None, :, :, None, None], s_loc.shape[:-1] + (1,))
    p = jax.nn.softmax(jnp.concatenate([s_loc, s_ctx, sink_col], axis=-1), axis=-1).astype(v.dtype)
    o = (jnp.einsum("bnkgqj,bnjkd->bnqkgd", p[..., :n_loc], vb)
         + jnp.einsum("bnkgqc,bckd->bnqkgd", p[..., n_loc:n_loc + lc], vc))
    o = o.reshape(b, s, hq * d)
    oc = ctx_attention(qc, kc, vc, sink) if ctx_out else None
    return o, oc


def dw_conv_centred(x, w, bias):
    left = (SSM_CONV - 1) // 2
    right = SSM_CONV - 1 - left
    y = lax.conv_general_dilated(x, w[:, None, :].astype(x.dtype), window_strides=(1,), padding=[(left, right)],
                                 dimension_numbers=("NWC", "WIO", "NWC"), feature_group_count=x.shape[-1])
    return y + bias


def ssd_scan(x, dt, a, bm, cm, h0):
    b, l, h, p = x.shape
    g, n = bm.shape[2], bm.shape[3]
    q = SSM_CHUNK
    nc = l // q
    bh = jnp.repeat(bm.astype(F32), h // g, axis=2).reshape(b, nc, q, h, n)
    ch = jnp.repeat(cm.astype(F32), h // g, axis=2).reshape(b, nc, q, h, n)
    xdt = (x.astype(F32) * dt[..., None]).reshape(b, nc, q, h, p)
    a_cum = jnp.cumsum((dt * a).reshape(b, nc, q, h).transpose(0, 1, 3, 2), axis=-1)
    lower = jnp.tril(jnp.ones((q, q), dtype=bool))
    seg = jnp.exp(jnp.where(lower, a_cum[..., :, None] - a_cum[..., None, :], -jnp.inf))
    scores = jnp.einsum("bcqhn,bcshn->bchqs", ch, bh) * seg
    y_diag = jnp.einsum("bchqs,bcshp->bcqhp", scores, xdt)
    decay_to_end = jnp.exp(a_cum[..., -1:] - a_cum)
    states = jnp.einsum("bcshn,bchs,bcshp->bchpn", bh, decay_to_end, xdt)
    chunk_decay = jnp.exp(a_cum[..., -1])

    def step(hs, inp):
        s_c, d_c = inp
        return hs * d_c[..., None, None] + s_c, hs

    h_last, h_in = lax.scan(step, h0.astype(F32), (jnp.moveaxis(states, 1, 0), jnp.moveaxis(chunk_decay, 1, 0)))
    h_in = jnp.moveaxis(h_in, 0, 1)
    y_off = jnp.einsum("bcqhn,bchpn,bchq->bcqhp", ch, h_in, jnp.exp(a_cum))
    return (y_diag + y_off).reshape(b, l, h, p), h_last


def ssd_mixer(z, xbc, dt_raw, zc, xbcc, dtc_raw, conv_w, conv_b, a_log, dt_bias, d_skip, norm_g, ctx_out):
    def prep(t):
        u = jax.nn.silu(dw_conv_centred(t, conv_w, conv_b))
        b, l = u.shape[0], u.shape[1]
        nb_ = SSM_GROUPS * SSM_STATE
        xs = u[..., :GROUP_WIDTH].reshape(b, l, SSM_HEADS, SSM_HEAD_DIM)
        bm = u[..., GROUP_WIDTH:GROUP_WIDTH + nb_].reshape(b, l, SSM_GROUPS, SSM_STATE)
        cm = u[..., GROUP_WIDTH + nb_:].reshape(b, l, SSM_GROUPS, SSM_STATE)
        return xs, bm, cm

    xs, bm, cm = prep(xbc)
    xsc, bc, cc = prep(xbcc)
    a = -jnp.exp(a_log.astype(F32))

    def dt_of(raw, i):
        return jax.nn.softplus(raw.astype(F32) + dt_bias[i].astype(F32))

    def flip(t):
        return jnp.flip(t, axis=1)

    h0 = jnp.zeros((xs.shape[0], SSM_HEADS, SSM_HEAD_DIM, SSM_STATE), F32)
    yc_f, hc_f = ssd_scan(xsc, dt_of(dtc_raw, 0), a[0], bc, cc, h0)
    yc_b, hc_b = ssd_scan(flip(xsc), flip(dt_of(dtc_raw, 1)), a[1], flip(bc), flip(cc), h0)
    y_f, _ = ssd_scan(xs, dt_of(dt_raw, 0), a[0], bm, cm, hc_f)
    y_b, _ = ssd_scan(flip(xs), flip(dt_of(dt_raw, 1)), a[1], flip(bm), flip(cm), hc_b)

    def finish(yf, yb_rev, xs_t, z_t):
        y = yf + flip(yb_rev) + d_skip.astype(F32)[:, None] * xs_t.astype(F32)
        y = y.reshape(y.shape[0], y.shape[1], GROUP_WIDTH)
        return rmsnorm(y * jax.nn.silu(z_t.astype(F32)), norm_g).astype(z_t.dtype)

    o = finish(y_f, y_b, xs, z)
    oc = finish(yc_f, yc_b, xsc, zc) if ctx_out else None
    return o, oc


def chunk_gmlp(uv, uvc, ln_g, ln_b, ws, bs, ctx_out):
    def apply(t):
        t = jax.nn.gelu(t)
        u, v = t[..., :GROUP_WIDTH], t[..., GROUP_WIDTH:]
        v = layernorm(v, ln_g, ln_b)
        b, l, w = v.shape
        vg = v.reshape(b, l // GM_CHUNK, GM_CHUNK, GM_GROUPS, w // GM_GROUPS)
        mixed = jnp.einsum("gij,bnjgc->bnigc", ws.astype(v.dtype), vg) + bs.T.astype(v.dtype)[None, None, :, :, None]
        return u * mixed.reshape(b, l, w)

    o = apply(uv)
    oc = apply(uvc) if ctx_out else None
    return o, oc


def moe(h, router_w, router_b, w_gate, b_gate, w_up, b_up, w_down, b_down):
    logits = (h @ router_w).astype(F32) + router_b.astype(F32)
    top_val, top_idx = lax.top_k(logits, TOP_K)
    gates = jax.nn.softmax(top_val, axis=-1)
    combine = jnp.sum(jax.nn.one_hot(top_idx, N_EXPERTS, dtype=F32) * gates[..., None], axis=1)
    out = jnp.zeros(h.shape, F32)
    for e in range(N_EXPERTS):
        g = jnp.minimum(h @ w_gate[e] + b_gate[e], SWIGLU_LIMIT)
        u = jnp.clip(h @ w_up[e] + b_up[e], -SWIGLU_LIMIT, SWIGLU_LIMIT)
        y = ((u + 1) * (g * jax.nn.sigmoid(SWIGLU_ALPHA * g))) @ w_down[e] + b_down[e]
        out = out + combine[:, e:e + 1] * y.astype(F32)
    return out.astype(h.dtype)


def hybrid_layer(x, xc, c, c_ctx, w_mod, b_mod, norm_g, w_in, w_out, na_rpb, wa_sink,
                 ssm_conv_w, ssm_conv_b, ssm_a_log, ssm_dt_bias, ssm_d, ssm_norm_g,
                 gm_ln_g, gm_ln_b, gm_ws, gm_bs, router_w, router_b,
                 exp_w_gate, exp_b_gate, exp_w_up, exp_b_up, exp_w_down, exp_b_down, ctx_out):
    b, s, d = x.shape
    lc = xc.shape[1]
    mod = (jax.nn.silu(c) @ w_mod + b_mod).reshape(b, N_MOD, 1, d)
    modc = (jax.nn.silu(c_ctx) @ w_mod + b_mod).reshape(N_MOD, 1, 1, d)
    ml = [mod[:, i] for i in range(N_MOD)]
    mc = [modc[i] for i in range(N_MOD)]

    h = rmsnorm(x, norm_g[0]) * (1 + ml[1]) + ml[0]
    hc = rmsnorm(xc, norm_g[0]) * (1 + mc[1]) + mc[0]
    aq, ak, av, bq, bk, bv, cz, cxbc, cdt, duv = split_cols(h @ w_in)
    aqc, akc, avc, bqc, bkc, bvc, czc, cxbcc, cdtc, duvc = split_cols(hc @ w_in)

    o_a, o_ac = neighborhood_attention(heads(aq, NA_HEADS), heads(ak, NA_HEADS), heads(av, NA_HEADS),
                                       heads(aqc, NA_HEADS), heads(akc, NA_HEADS), heads(avc, NA_HEADS),
                                       na_rpb, ctx_out)
    o_b, o_bc = window_gqa(axial_rope(heads(bq, WA_HEADS)), axial_rope(heads(bk, WA_KV_HEADS)), heads(bv, WA_KV_HEADS),
                           heads(bqc, WA_HEADS), heads(bkc, WA_KV_HEADS), heads(bvc, WA_KV_HEADS),
                           wa_sink, ctx_out)
    o_c, o_cc = ssd_mixer(cz, cxbc, cdt, czc, cxbcc, cdtc, ssm_conv_w, ssm_conv_b, ssm_a_log, ssm_dt_bias,
                          ssm_d, ssm_norm_g, ctx_out)
    o_d, o_dc = chunk_gmlp(duv, duvc, gm_ln_g, gm_ln_b, gm_ws, gm_bs, ctx_out)

    y = jnp.concatenate([o_a, o_b, o_c, o_d], axis=-1) @ w_out
    x = x + ml[2] * rmsnorm(y, norm_g[1])
    if ctx_out:
        yc = jnp.concatenate([o_ac, o_bc, o_cc, o_dc], axis=-1) @ w_out
        xc = xc + mc[2] * rmsnorm(yc, norm_g[1])

    tokens = (rmsnorm(x, norm_g[2]) * (1 + ml[4]) + ml[3]).reshape(b * s, d)
    if ctx_out:
        tok_c = (rmsnorm(xc, norm_g[2]) * (1 + mc[4]) + mc[3]).reshape(b * lc, d)
        tokens = jnp.concatenate([tokens, tok_c], axis=0)
    f = moe(tokens, router_w, router_b, exp_w_gate, exp_b_gate, exp_w_up, exp_b_up, exp_w_down, exp_b_down)
    x = x + ml[5] * rmsnorm(f[:b * s].reshape(b, s, d), norm_g[3])
    if ctx_out:
        xc = xc + mc[5] * rmsnorm(f[b * s:].reshape(b, lc, d), norm_g[3])
    return x, xc


def setup_inputs(seed: int = 0) -> dict:
    key = jax.random.key(seed)
    ks = jax.random.split(key, 32)

    def nrm(k, shape, scale):
        return jax.random.normal(k, shape, F32) * scale

    dt0 = jnp.exp(jax.random.uniform(ks[13], (DEPTH, 2, SSM_HEADS), F32, math.log(1e-3), math.log(1e-1)))
    return {
        "x": nrm(ks[0], (BATCH, SEQ, D_MODEL), 1.0),
        "c": nrm(ks[1], (BATCH, D_MODEL), 1.0),
        "ctx": nrm(ks[2], (BATCH, CTX_LEN, D_MODEL), 1.0),
        "c_ctx": nrm(ks[3], (D_MODEL,), 1.0),
        "w_mod": nrm(ks[4], (DEPTH, D_MODEL, N_MOD * D_MODEL), 0.5 * D_MODEL ** -0.5),
        "b_mod": nrm(ks[5], (DEPTH, N_MOD * D_MODEL), 0.02),
        "norm_g": 1.0 + nrm(ks[6], (DEPTH, 4, D_MODEL), 0.05),
        "w_in": nrm(ks[7], (DEPTH, D_MODEL, IN_COLS), D_MODEL ** -0.5),
        "w_out": nrm(ks[8], (DEPTH, MIX_WIDTH, D_MODEL), MIX_WIDTH ** -0.5),
        "na_rpb": nrm(ks[9], (DEPTH, NA_HEADS, 2 * NA_WIN_ROWS - 1, 2 * NA_WIN_COLS - 1), 0.5),
        "wa_sink": nrm(ks[10], (DEPTH, WA_HEADS), 0.5),
        "ssm_conv_w": nrm(ks[11], (DEPTH, SSM_CONV, SSM_CONV_CH), SSM_CONV ** -0.5),
        "ssm_conv_b": nrm(ks[12], (DEPTH, SSM_CONV_CH), 0.02),
        "ssm_a_log": jnp.log(jax.random.uniform(ks[14], (DEPTH, 2, SSM_HEADS), F32, 1.0, 16.0)),
        "ssm_dt_bias": dt0 + jnp.log(-jnp.expm1(-dt0)),
        "ssm_d": 1.0 + nrm(ks[15], (DEPTH, SSM_HEADS), 0.1),
        "ssm_norm_g": 1.0 + nrm(ks[16], (DEPTH, GROUP_WIDTH), 0.05),
        "gm_ln_g": 1.0 + nrm(ks[17], (DEPTH, GROUP_WIDTH), 0.05),
        "gm_ln_b": nrm(ks[18], (DEPTH, GROUP_WIDTH), 0.02),
        "gm_ws": nrm(ks[19], (DEPTH, GM_GROUPS, GM_CHUNK, GM_CHUNK), GM_CHUNK ** -0.5),
        "gm_bs": nrm(ks[20], (DEPTH, GM_GROUPS, GM_CHUNK), 0.02),
        "router_w": nrm(ks[21], (DEPTH, D_MODEL, N_EXPERTS), D_MODEL ** -0.5),
        "router_b": nrm(ks[22], (DEPTH, N_EXPERTS), 0.01),
        "exp_w_gate": nrm(ks[23], (DEPTH, N_EXPERTS, D_MODEL, EXPERT_FF), D_MODEL ** -0.5),
        "exp_b_gate": nrm(ks[24], (DEPTH, N_EXPERTS, EXPERT_FF), 0.02),
        "exp_w_up": nrm(ks[25], (DEPTH, N_EXPERTS, D_MODEL, EXPERT_FF), D_MODEL ** -0.5),
        "exp_b_up": nrm(ks[26], (DEPTH, N_EXPERTS, EXPERT_FF), 0.02),
        "exp_w_down": nrm(ks[27], (DEPTH, N_EXPERTS, EXPERT_FF, D_MODEL), EXPERT_FF ** -0.5),
        "exp_b_down": nrm(ks[28], (DEPTH, N_EXPERTS, D_MODEL), 0.02),
    }


def reference(x, c, ctx, c_ctx, w_mod, b_mod, norm_g, w_in, w_out, na_rpb, wa_sink,
              ssm_conv_w, ssm_conv_b, ssm_a_log, ssm_dt_bias, ssm_d, ssm_norm_g,
              gm_ln_g, gm_ln_b, gm_ws, gm_bs, router_w, router_b,
              exp_w_gate, exp_b_gate, exp_w_up, exp_b_up, exp_w_down, exp_b_down):
    xc = ctx
    for l in range(DEPTH):
        x, xc = hybrid_layer(x, xc, c, c_ctx, w_mod[l], b_mod[l], norm_g[l], w_in[l], w_out[l], na_rpb[l], wa_sink[l],
                             ssm_conv_w[l], ssm_conv_b[l], ssm_a_log[l], ssm_dt_bias[l], ssm_d[l], ssm_norm_g[l],
                             gm_ln_g[l], gm_ln_b[l], gm_ws[l], gm_bs[l], router_w[l], router_b[l],
                             exp_w_gate[l], exp_b_gate[l], exp_w_up[l], exp_b_up[l], exp_w_down[l], exp_b_down[l],
                             ctx_out=(l < DEPTH - 1))
    return x
```

```python
import functools
import math

import numpy as np
import jax
import jax.numpy as jnp
from jax import lax
from jax.experimental import pallas as pl
from jax.experimental.pallas import tpu as pltpu

F32 = jnp.float32
BF16 = jnp.bfloat16
I32 = jnp.int32

LANES = 128
TM = 256
GRID_W = 64
HEAD_DIM = 64
GROUP_W = 512
NA_WIN_ROWS = 8
NA_WIN_COLS = 16
NA_QROWS = 4
NA_KROWS = NA_QROWS + NA_WIN_ROWS - 1
WA_WINDOW = 128
ROPE_BASE = 10000.0
SSM_HEADS = 8
SSM_CHUNK = 128
SSM_STATE = 128
N_EXPERTS = 32
TOP_K = 4
EXPERT_FF = 1024
SWIGLU_LIMIT = 7.0
SWIGLU_ALPHA = 1.702
NORM_EPS = 1e-6
NEG_INF = -1e30

COLS_A = 3 * GROUP_W
COLS_B = GROUP_W + 4 * LANES
COLS_C = 2 * GROUP_W + GROUP_W + LANES
COLS_D = 2 * GROUP_W
IN_COLS_P = COLS_A + COLS_B + COLS_C + COLS_D

VMEM_LIMIT = 48 * 1024 * 1024


def _cparams(sem):
    return pltpu.CompilerParams(dimension_semantics=sem, vmem_limit_bytes=VMEM_LIMIT)


def _dot(a, b):
    return jnp.dot(a, b, preferred_element_type=F32)


def _dot_nt(a, b):
    return lax.dot_general(a, b, (((1,), (1,)), ((), ())), preferred_element_type=F32)


def _sigmoid(x):
    return 1.0 / (1.0 + jnp.exp(-x))


def _rms(x, g):
    return x * lax.rsqrt(jnp.mean(x * x, axis=-1, keepdims=True) + NORM_EPS) * g


def _lane_iota(shape):
    return lax.broadcasted_iota(I32, shape, len(shape) - 1)


def _mod_kernel(c_ref, w_ref, b_ref, o_ref):
    c = c_ref[...]
    a = (c * _sigmoid(c)).astype(BF16)
    o_ref[...] = _dot(a, w_ref[...].astype(BF16)) + b_ref[...]


def _modulation(cvec, w_mod, b_mod):
    rows, d = cvec.shape
    ncol = w_mod.shape[1]
    bn = 1024
    return pl.pallas_call(
        _mod_kernel,
        grid=(ncol // bn,),
        in_specs=[pl.BlockSpec((rows, d), lambda j: (0, 0)),
                  pl.BlockSpec((d, bn), lambda j: (0, j)),
                  pl.BlockSpec((1, bn), lambda j: (0, j))],
        out_specs=pl.BlockSpec((rows, bn), lambda j: (0, j)),
        out_shape=jax.ShapeDtypeStruct((rows, ncol), F32),
        compiler_params=_cparams(("arbitrary",)),
        name="modulation",
    )(cvec, w_mod, b_mod.reshape(1, ncol))


def _rope(x, cos, sin):
    lane = _lane_iota((1, LANES))
    first = (lane & 31) < 16
    partner = jnp.where(first, pltpu.roll(x, LANES - 16, 1), pltpu.roll(x, 16, 1))
    return x * cos + partner * sin


def _inproj_kernel(x_ref, mod_ref, g_ref, w_ref, cos_ref, sin_ref, a_ref, b_ref, c_ref, d_ref):
    h = (_rms(x_ref[...], g_ref[...]) * (1.0 + mod_ref[0, 1:2, :]) + mod_ref[0, 0:1, :]).astype(BF16)
    a_ref[...] = _dot(h, w_ref[:, 0:COLS_A]).astype(BF16)
    cos = cos_ref[...]
    sin = sin_ref[...]
    n_rope = (GROUP_W + 2 * LANES) // LANES
    for i in range(COLS_B // LANES):
        c0 = COLS_A + i * LANES
        blk = _dot(h, w_ref[:, c0:c0 + LANES])
        if i < n_rope:
            blk = _rope(blk, cos, sin)
        b_ref[:, i * LANES:(i + 1) * LANES] = blk.astype(BF16)
    c0 = COLS_A + COLS_B
    c_ref[...] = _dot(h, w_ref[:, c0:c0 + COLS_C])
    c0 += COLS_C
    d_ref[...] = _dot(h, w_ref[:, c0:c0 + COLS_D])


def _inproj(x_all, mod_tab, g, w_in_p, cos_tab, sin_tab, tpb, nb):
    n, d = x_all.shape
    nt = n // TM

    def mod_map(i):
        return (jnp.where(i % tpb == 0, nb, i // tpb), 0, 0)

    return pl.pallas_call(
        _inproj_kernel,
        grid=(nt,),
        in_specs=[pl.BlockSpec((TM, d), lambda i: (i, 0)),
                  pl.BlockSpec((1, 6, d), mod_map),
                  pl.BlockSpec((1, d), lambda i: (0, 0)),
                  pl.BlockSpec((d, IN_COLS_P), lambda i: (0, 0), pipeline_mode=pl.Buffered(1)),
                  pl.BlockSpec((TM, LANES), lambda i: (i % tpb, 0)),
                  pl.BlockSpec((TM, LANES), lambda i: (i % tpb, 0))],
        out_specs=[pl.BlockSpec((TM, COLS_A), lambda i: (i, 0)),
                   pl.BlockSpec((TM, COLS_B), lambda i: (i, 0)),
                   pl.BlockSpec((TM, COLS_C), lambda i: (i, 0)),
                   pl.BlockSpec((TM, COLS_D), lambda i: (i, 0))],
        out_shape=[jax.ShapeDtypeStruct((n, COLS_A), BF16),
                   jax.ShapeDtypeStruct((n, COLS_B), BF16),
                   jax.ShapeDtypeStruct((n, COLS_C), F32),
                   jax.ShapeDtypeStruct((n, COLS_D), F32)],
        compiler_params=_cparams(("arbitrary",)),
        name="inproj",
    )(x_all, mod_tab, g.reshape(1, d), w_in_p, cos_tab, sin_tab)


def _two_head_attention(q, parts, sinks):
    lane = _lane_iota((1, LANES))
    m0 = lane < HEAD_DIM
    zero = jnp.zeros((), BF16)
    out = None
    for h in range(2):
        mh = m0 if h == 0 else jnp.logical_not(m0)
        qh = jnp.where(mh, q, zero)
        scores = []
        for (k, v, bias0, bias1) in parts:
            s = _dot_nt(qh, k)
            bias = bias0 if h == 0 else bias1
            if bias is not None:
                s = s + bias
            scores.append(s)
        m = scores[0].max(axis=-1, keepdims=True)
        for s in scores[1:]:
            m = jnp.maximum(m, s.max(axis=-1, keepdims=True))
        if sinks is not None:
            m = jnp.maximum(m, sinks[h])
        l = None
        o = None
        for s, (k, v, _, _) in zip(scores, parts):
            p = jnp.exp(s - m)
            ls = p.sum(axis=-1, keepdims=True)
            l = ls if l is None else l + ls
            oh = _dot(p.astype(BF16), jnp.where(mh, v, zero))
            o = oh if o is None else o + oh
        if sinks is not None:
            l = l + jnp.exp(sinks[h] - m)
        o = o * (1.0 / l)
        out = o if out is None else out + o
    return out


def _na_kernel(q_ref, k_ref, v_ref, bias_ref, o_ref, *, lc, rows):
    j = pl.program_id(2)
    q = q_ref[0]
    kc = k_ref[0, 0:lc, :]
    vc = v_ref[0, 0:lc, :]
    nblk = rows // NA_QROWS

    @pl.when(j == 0)
    def _():
        o_ref[0] = _two_head_attention(q, [(kc, vc, None, None)], None).astype(BF16)

    @pl.when(j > 0)
    def _():
        g = j - 1
        sr = jnp.clip(NA_QROWS * g - NA_WIN_ROWS // 2, 0, rows - NA_KROWS)
        start = pl.multiple_of(lc + sr * GRID_W, GRID_W)
        nk = NA_KROWS * GRID_W
        kw = k_ref[0, pl.ds(start, nk), :]
        vw = v_ref[0, pl.ds(start, nk), :]
        var = jnp.where(g == 0, 0, jnp.where(g == nblk - 1, 2, 1))
        b0 = bias_ref[var, 0]
        b1 = bias_ref[var, 1]
        o_ref[0] = _two_head_attention(q, [(kw, vw, b0, b1), (kc, vc, None, None)], None).astype(BF16)


def _na_attention(qkv, bias, nb, sa, lc):
    rows = (sa - lc) // GRID_W
    npair = GROUP_W // LANES
    kern = functools.partial(_na_kernel, lc=lc, rows=rows)
    nkb = NA_KROWS * GRID_W
    return pl.pallas_call(
        kern,
        grid=(npair, nb, sa // TM),
        in_specs=[pl.BlockSpec((1, TM, LANES), lambda p, b, j: (b, j, p)),
                  pl.BlockSpec((1, sa, LANES), lambda p, b, j: (b, 0, npair + p)),
                  pl.BlockSpec((1, sa, LANES), lambda p, b, j: (b, 0, 2 * npair + p)),
                  pl.BlockSpec((3, 2, TM, nkb), lambda p, b, j: (0, p, 0, 0))],
        out_specs=pl.BlockSpec((1, TM, LANES), lambda p, b, j: (b, j, p)),
        out_shape=jax.ShapeDtypeStruct((nb, sa, GROUP_W), BF16),
        compiler_params=_cparams(("arbitrary", "arbitrary", "arbitrary")),
        name="na_attention",
    )(qkv, qkv, qkv, bias)


def _wa_kernel(sink_ref, q_ref, k_ref, v_ref, o_ref, *, lc, sa):
    p = pl.program_id(0)
    j = pl.program_id(2)
    q = q_ref[0]
    kc = k_ref[0, 0:lc, :]
    vc = v_ref[0, 0:lc, :]
    sinks = (sink_ref[2 * p], sink_ref[2 * p + 1])

    @pl.when(j == 0)
    def _():
        o_ref[0] = _two_head_attention(q, [(kc, vc, None, None)], sinks).astype(BF16)

    @pl.when(j > 0)
    def _():
        g = j - 1
        nk = TM + 2 * WA_WINDOW
        start = pl.multiple_of(jnp.minimum(lc + TM * g - WA_WINDOW, sa - nk), WA_WINDOW)
        kw = k_ref[0, pl.ds(start, nk), :]
        vw = v_ref[0, pl.ds(start, nk), :]
        qpos = TM * g + lax.broadcasted_iota(I32, (TM, nk), 0)
        kpos = start - lc + lax.broadcasted_iota(I32, (TM, nk), 1)
        dist = qpos - kpos
        ok = (dist <= WA_WINDOW) & (dist >= -WA_WINDOW) & (kpos >= 0)
        band = jnp.where(ok, 0.0, NEG_INF).astype(F32)
        o_ref[0] = _two_head_attention(q, [(kw, vw, band, band), (kc, vc, None, None)], sinks).astype(BF16)


def _wa_attention(bmat, sink, nb, sa, lc):
    npair = GROUP_W // LANES
    kern = functools.partial(_wa_kernel, lc=lc, sa=sa)
    grid_spec = pltpu.PrefetchScalarGridSpec(
        num_scalar_prefetch=1,
        grid=(npair, nb, sa // TM),
        in_specs=[pl.BlockSpec((1, TM, LANES), lambda p, b, j, s: (b, j, p)),
                  pl.BlockSpec((1, sa, LANES), lambda p, b, j, s: (b, 0, npair + p // 2)),
                  pl.BlockSpec((1, sa, LANES), lambda p, b, j, s: (b, 0, npair + 2 + p // 2))],
        out_specs=pl.BlockSpec((1, TM, LANES), lambda p, b, j, s: (b, j, p)),
    )
    return pl.pallas_call(
        kern,
        grid_spec=grid_spec,
        out_shape=jax.ShapeDtypeStruct((nb, sa, GROUP_W), BF16),
        compiler_params=_cparams(("arbitrary", "arbitrary", "arbitrary")),
        name="wa_attention",
    )(sink, bmat, bmat, bmat)


def _pair_bcast(x, h0):
    m0 = _lane_iota((1, LANES)) < HEAD_DIM
    return jnp.where(m0, x[:, h0:h0 + 1], x[:, h0 + 1:h0 + 2])


def _ssd_kernel(*refs, backward, nchunk, lcc):
    if backward:
        (xbc_ref, prev_ref, next_ref, dt_ref, convw_ref, convb_ref, alog_ref, dtb_ref,
         yf_ref, z_ref, dskip_ref, ng_ref, o_ref, h_scr) = refs
    else:
        (xbc_ref, prev_ref, next_ref, dt_ref, convw_ref, convb_ref, alog_ref, dtb_ref, o_ref, h_scr) = refs
    q = SSM_CHUNK
    c = pl.program_id(1)
    if backward:
        cidx = jnp.where(c < lcc, lcc - 1 - c, nchunk + lcc - 1 - c)
    else:
        cidx = c

    @pl.when(c == 0)
    def _():
        h_scr[...] = jnp.zeros_like(h_scr)

    x = xbc_ref[0]
    row = lax.broadcasted_iota(I32, (q, 1), 0)
    zp = (cidx == 0) | (cidx == lcc)
    zn = (cidx == lcc - 1) | (cidx == nchunk - 1)
    pm1 = jnp.where(zp, 0.0, prev_ref[0, 7:8, :])
    n0 = jnp.where(zn, 0.0, next_ref[0, 0:1, :])
    n1 = jnp.where(zn, 0.0, next_ref[0, 1:2, :])
    xm1 = jnp.where(row == 0, pm1, pltpu.roll(x, 1, 0))
    xp1 = jnp.where(row == q - 1, n0, pltpu.roll(x, q - 1, 0))
    xp2 = jnp.where(row == q - 2, n0, jnp.where(row == q - 1, n1, pltpu.roll(x, q - 2, 0)))
    u = (convw_ref[0:1, :] * xm1 + convw_ref[1:2, :] * x + convw_ref[2:3, :] * xp1 + convw_ref[3:4, :] * xp2
         + convb_ref[...])
    u = u * _sigmoid(u)
    xs = u[:, 0:GROUP_W]

    dtr = dt_ref[0] + dtb_ref[...]
    dt = jnp.maximum(dtr, 0.0) + jnp.log1p(jnp.exp(-jnp.abs(dtr)))
    adt = dt * (-jnp.exp(alog_ref[...]))

    qi = lax.broadcasted_iota(I32, (q, q), 0)
    si = lax.broadcasted_iota(I32, (q, q), 1)
    keep = (si >= qi) if backward else (si <= qi)
    tri = jnp.where(keep, 1.0, 0.0).astype(BF16)
    a1 = adt.astype(BF16)
    r1 = adt - a1.astype(F32)
    a2 = r1.astype(BF16)
    a3 = (r1 - a2.astype(F32)).astype(BF16)
    acum = _dot(tri, a1) + _dot(tri, a2) + _dot(tri, a3)
    acum_t = acum.T
    tot_row = acum[0:1, :] if backward else acum[q - 1:q, :]
    dte = jnp.exp(tot_row - acum)
    ea = jnp.exp(acum)

    m0 = _lane_iota((1, LANES)) < HEAD_DIM
    ys = []
    for g in range(2):
        bg = u[:, GROUP_W + g * SSM_STATE:GROUP_W + (g + 1) * SSM_STATE].astype(BF16)
        cg = u[:, GROUP_W + 2 * SSM_STATE + g * SSM_STATE:GROUP_W + 2 * SSM_STATE + (g + 1) * SSM_STATE].astype(BF16)
        gmat = _dot_nt(cg, bg)
        hprev = h_scr[g * 256:(g + 1) * 256, :]
        yoff = _dot_nt(cg, hprev.astype(BF16))
        wparts = []
        for pp in range(2):
            pair = 2 * g + pp
            h0 = 2 * pair
            xdt = xs[:, pair * LANES:(pair + 1) * LANES] * _pair_bcast(dt, h0)
            yp = None
            for hh in range(2):
                h = h0 + hh
                diff = acum[:, h:h + 1] - acum_t[h:h + 1, :]
                seg = jnp.exp(jnp.where(keep, diff, NEG_INF))
                sc = (gmat * seg).astype(BF16)
                mh = m0 if hh == 0 else jnp.logical_not(m0)
                t = _dot(sc, jnp.where(mh, xdt, 0.0).astype(BF16))
                yp = t if yp is None else yp + t
            yp = yp + yoff[:, pp * LANES:(pp + 1) * LANES] * _pair_bcast(ea, h0)
            ys.append(yp)
            wparts.append(xdt * _pair_bcast(dte, h0))
        wg = jnp.concatenate(wparts, axis=1)
        states = _dot(wg.T.astype(BF16), bg)
        for hh in range(4):
            h = 4 * g + hh
            dec = jnp.exp(acum_t[h:h + 1, 0:1] if backward else acum_t[h:h + 1, q - 1:q])
            r0 = g * 256 + hh * HEAD_DIM
            h_scr[r0:r0 + HEAD_DIM, :] = (h_scr[r0:r0 + HEAD_DIM, :] * dec
                                          + states[hh * HEAD_DIM:(hh + 1) * HEAD_DIM, :])
    y = jnp.concatenate(ys, axis=1)
    if backward:
        y = yf_ref[0] + y + dskip_ref[...] * xs
        z = z_ref[0]
        y = y * (z * _sigmoid(z))
        o_ref[0] = _rms(y, ng_ref[...]).astype(BF16)
    else:
        o_ref[0] = y


def _ssd(cmat, conv_w, conv_b, a_log, dt_bias, d_skip, norm_g, nb, sa, lc):
    q = SSM_CHUNK
    nchunk = sa // q
    lcc = lc // q
    cw = 2 * GROUP_W
    n8 = sa // 8

    def lane_pad(v):
        return jnp.pad(v.reshape(1, -1), ((0, 0), (0, LANES - v.shape[-1])))

    def run(backward, extra_in, extra_specs, out_dtype):
        if backward:
            def cm(c):
                return jnp.where(c < lcc, lcc - 1 - c, nchunk + lcc - 1 - c)
        else:
            def cm(c):
                return c
        kern = functools.partial(_ssd_kernel, backward=backward, nchunk=nchunk, lcc=lcc)
        d = 1 if backward else 0
        in_specs = [pl.BlockSpec((1, q, cw), lambda b, c: (b, cm(c), 0)),
                    pl.BlockSpec((1, 8, cw), lambda b, c: (b, jnp.maximum(cm(c) * (q // 8) - 1, 0), 0)),
                    pl.BlockSpec((1, 8, cw), lambda b, c: (b, jnp.minimum((cm(c) + 1) * (q // 8), n8 - 1), 0)),
                    pl.BlockSpec((1, q, LANES), lambda b, c: (b, cm(c), (cw + GROUP_W) // LANES)),
                    pl.BlockSpec((4, cw), lambda b, c: (0, 0)),
                    pl.BlockSpec((1, cw), lambda b, c: (0, 0)),
                    pl.BlockSpec((1, LANES), lambda b, c: (0, 0)),
                    pl.BlockSpec((1, LANES), lambda b, c: (0, 0))] + extra_specs(cm)
        return pl.pallas_call(
            kern,
            grid=(nb, nchunk),
            in_specs=in_specs,
            out_specs=pl.BlockSpec((1, q, GROUP_W), lambda b, c: (b, cm(c), 0)),
            out_shape=jax.ShapeDtypeStruct((nb, sa, GROUP_W), out_dtype),
            scratch_shapes=[pltpu.VMEM((SSM_HEADS * HEAD_DIM, SSM_STATE), F32)],
            compiler_params=_cparams(("arbitrary", "arbitrary")),
            name="ssd_bwd" if backward else "ssd_fwd",
        )(cmat, cmat, cmat, cmat, conv_w, conv_b.reshape(1, cw), lane_pad(a_log[d]), lane_pad(dt_bias[d]),
          *extra_in)

    y_f = run(False, (), lambda cm: [], F32)
    extra_specs = lambda cm: [pl.BlockSpec((1, q, GROUP_W), lambda b, c: (b, cm(c), 0)),
                              pl.BlockSpec((1, q, GROUP_W), lambda b, c: (b, cm(c), cw // GROUP_W)),
                              pl.BlockSpec((1, GROUP_W), lambda b, c: (0, 0)),
                              pl.BlockSpec((1, GROUP_W), lambda b, c: (0, 0))]
    dskip_row = jnp.repeat(d_skip, HEAD_DIM).reshape(1, GROUP_W)
    return run(True, (y_f, cmat, dskip_row, norm_g.reshape(1, GROUP_W)), extra_specs, BF16)


def _gmlp_kernel(x_ref, lng_ref, lnb_ref, ws_ref, bias_ref, o_ref):
    t = x_ref[...]
    t = 0.5 * t * (1.0 + jnp.tanh(math.sqrt(2.0 / math.pi) * (t + 0.044715 * (t * t * t))))
    u = t[:, 0:GROUP_W]
    v = t[:, GROUP_W:2 * GROUP_W]
    mu = jnp.mean(v, axis=-1, keepdims=True)
    vc = v - mu
    var = jnp.mean(vc * vc, axis=-1, keepdims=True)
    vn = vc * lax.rsqrt(var + NORM_EPS) * lng_ref[...] + lnb_ref[...]
    m0 = _lane_iota((1, LANES)) < HEAD_DIM
    q = SSM_CHUNK
    for cc in range(TM // q):
        parts = []
        for p in range(GROUP_W // LANES):
            vp = vn[cc * q:(cc + 1) * q, p * LANES:(p + 1) * LANES]
            lo = jnp.where(m0, vp, 0.0).astype(BF16)
            hi = jnp.where(m0, 0.0, vp).astype(BF16)
            parts.append(_dot(ws_ref[2 * p], lo) + _dot(ws_ref[2 * p + 1], hi))
        mixed = jnp.concatenate(parts, axis=1) + bias_ref[...]
        o_ref[cc * q:(cc + 1) * q, :] = (u[cc * q:(cc + 1) * q, :] * mixed).astype(BF16)


def _gmlp(dmat, ln_g, ln_b, ws, bs):
    n = dmat.shape[0]
    ngrp = ws.shape[0]
    bias = jnp.repeat(bs.T, GROUP_W // ngrp, axis=1)
    return pl.pallas_call(
        _gmlp_kernel,
        grid=(n // TM,),
        in_specs=[pl.BlockSpec((TM, 2 * GROUP_W), lambda i: (i, 0)),
                  pl.BlockSpec((1, GROUP_W), lambda i: (0, 0)),
                  pl.BlockSpec((1, GROUP_W), lambda i: (0, 0)),
                  pl.BlockSpec((ngrp, SSM_CHUNK, SSM_CHUNK), lambda i: (0, 0, 0)),
                  pl.BlockSpec((SSM_CHUNK, GROUP_W), lambda i: (0, 0))],
        out_specs=pl.BlockSpec((TM, GROUP_W), lambda i: (i, 0)),
        out_shape=jax.ShapeDtypeStruct((n, GROUP_W), BF16),
        compiler_params=_cparams(("arbitrary",)),
        name="gmlp",
    )(dmat, ln_g.reshape(1, -1), ln_b.reshape(1, -1), ws.astype(BF16), bias)


def _outproj_kernel(oa_ref, ob_ref, oc_ref, od_ref, x_ref, mod_ref, g_ref, w_ref, rw_ref, rb_ref,
                    xo_ref, tok_ref, sel_ref, idx_ref, gate_ref):
    y = (_dot(oa_ref[...], w_ref[0:GROUP_W, :]) + _dot(ob_ref[...], w_ref[GROUP_W:2 * GROUP_W, :])
         + _dot(oc_ref[...], w_ref[2 * GROUP_W:3 * GROUP_W, :]) + _dot(od_ref[...], w_ref[3 * GROUP_W:4 * GROUP_W, :]))
    xn = x_ref[...] + mod_ref[0, 2:3, :] * _rms(y, g_ref[1:2, :])
    xo_ref[...] = xn
    tok = _rms(xn, g_ref[2:3, :]) * (1.0 + mod_ref[0, 4:5, :]) + mod_ref[0, 3:4, :]
    tok_ref[...] = tok
    logits = _dot(tok.astype(BF16), rw_ref[...]) + rb_ref[...]
    lane = _lane_iota(logits.shape)
    lane_f = lane.astype(F32)
    sel = jnp.zeros(logits.shape, F32)
    gates = jnp.zeros(logits.shape, F32)
    idx4 = jnp.zeros(logits.shape, I32)
    top = None
    den = None
    es = []
    for k in range(TOP_K):
        m = logits.max(axis=-1, keepdims=True)
        idx = jnp.where(logits == m, lane_f, float(LANES)).min(axis=-1, keepdims=True)
        hit = lane_f == idx
        if k == 0:
            top = m
        e = jnp.exp(m - top)
        den = e if den is None else den + e
        es.append(e)
        sel = jnp.where(hit, 1.0, sel)
        idx4 = jnp.where(lane == k, idx.astype(I32), idx4)
        logits = jnp.where(hit, -3e38, logits)
    inv = 1.0 / den
    for k in range(TOP_K):
        gates = jnp.where(lane == k, es[k] * inv, gates)
    sel_ref[...] = sel.astype(BF16)
    idx_ref[...] = idx4
    gate_ref[...] = gates


def _outproj(oa, ob, oc, od, x_all, mod_tab, g, w_out, rw_p, rb_p, tpb, nb):
    n, d = x_all.shape

    def mod_map(i):
        return (jnp.where(i % tpb == 0, nb, i // tpb), 0, 0)

    row = lambda i: (i, 0)
    const = lambda i: (0, 0)
    return pl.pallas_call(
        _outproj_kernel,
        grid=(n // TM,),
        in_specs=[pl.BlockSpec((TM, GROUP_W), row)] * 4 + [
            pl.BlockSpec((TM, d), row),
            pl.BlockSpec((1, 6, d), mod_map),
            pl.BlockSpec((4, d), const),
            pl.BlockSpec((4 * GROUP_W, d), const, pipeline_mode=pl.Buffered(1)),
            pl.BlockSpec((d, LANES), const),
            pl.BlockSpec((1, LANES), const)],
        out_specs=[pl.BlockSpec((TM, d), row), pl.BlockSpec((TM, d), row),
                   pl.BlockSpec((TM, LANES), row), pl.BlockSpec((TM, LANES), row), pl.BlockSpec((TM, LANES), row)],
        out_shape=[jax.ShapeDtypeStruct((n, d), F32), jax.ShapeDtypeStruct((n, d), F32),
                   jax.ShapeDtypeStruct((n, LANES), BF16), jax.ShapeDtypeStruct((n, LANES), I32),
                   jax.ShapeDtypeStruct((n, LANES), F32)],
        compiler_params=_cparams(("arbitrary",)),
        name="outproj_router",
    )(oa, ob, oc, od, x_all, mod_tab, g, w_out, rw_p, rb_p)


def _rank_kernel(sel_ref, rank_ref, cnt_ref, run_scr):
    i = pl.program_id(0)

    @pl.when(i == 0)
    def _():
        run_scr[...] = jnp.zeros_like(run_scr)

    sel = sel_ref[...]
    r = lax.broadcasted_iota(I32, (TM, TM), 0)
    c = lax.broadcasted_iota(I32, (TM, TM), 1)
    strict = jnp.where(c < r, 1.0, 0.0).astype(BF16)
    run = run_scr[0:1, :]
    rank_ref[...] = _dot(strict, sel) + run
    new = run + sel.astype(F32).sum(axis=0, keepdims=True)
    run_scr[...] = jnp.broadcast_to(new, run_scr.shape)
    cnt_ref[...] = jnp.broadcast_to(new, cnt_ref.shape)


def _pos_kernel(rank_ref, idx_ref, cnt_ref, pos_ref, meta_ref):
    cnt = cnt_ref[...].astype(I32)
    padded = jnp.right_shift(cnt + (TM - 1), TM.bit_length() - 1) * TM
    lane = _lane_iota(padded.shape)
    incl = padded
    s = 1
    while s < N_EXPERTS:
        incl = incl + jnp.where(lane >= s, pltpu.roll(incl, s, 1), 0)
        s *= 2
    offs = incl - padded
    base = (offs[0:1, :].astype(F32) + rank_ref[...])
    idx4 = idx_ref[...]
    lane_t = _lane_iota(base.shape)
    pos = jnp.zeros(base.shape, I32)
    for k in range(TOP_K):
        hit = lane_t == idx4[:, k:k + 1]
        pk = jnp.where(hit, base, 0.0).sum(axis=-1, keepdims=True).astype(I32)
        pos = jnp.where(lane_t == k, pk, pos)
    pos_ref[...] = pos
    row = lax.broadcasted_iota(I32, padded.shape, 0)
    meta_ref[...] = jnp.where(row == 0, offs, padded)


def _route(sel, idx4):
    n = sel.shape[0]
    rank, cnt = pl.pallas_call(
        _rank_kernel,
        grid=(n // TM,),
        in_specs=[pl.BlockSpec((TM, LANES), lambda i: (i, 0))],
        out_specs=[pl.BlockSpec((TM, LANES), lambda i: (i, 0)), pl.BlockSpec((8, LANES), lambda i: (0, 0))],
        out_shape=[jax.ShapeDtypeStruct((n, LANES), F32), jax.ShapeDtypeStruct((8, LANES), F32)],
        scratch_shapes=[pltpu.VMEM((8, LANES), F32)],
        compiler_params=_cparams(("arbitrary",)),
        name="route_rank",
    )(sel)
    pos, meta = pl.pallas_call(
        _pos_kernel,
        grid=(n // TM,),
        in_specs=[pl.BlockSpec((TM, LANES), lambda i: (i, 0)), pl.BlockSpec((TM, LANES), lambda i: (i, 0)),
                  pl.BlockSpec((8, LANES), lambda i: (0, 0))],
        out_specs=[pl.BlockSpec((TM, LANES), lambda i: (i, 0)), pl.BlockSpec((8, LANES), lambda i: (0, 0))],
        out_shape=[jax.ShapeDtypeStruct((n, LANES), I32), jax.ShapeDtypeStruct((8, LANES), I32)],
        compiler_params=_cparams(("arbitrary",)),
        name="route_pos",
    )(rank, idx4, cnt)
    return pos, meta


def _dispatch_kernel(pos_ref, lts_ref, lth_ref, tok_ref, xs_ref, zero_scr, sem, semz):
    i = pl.program_id(0)

    @pl.when(i == 0)
    def _():
        zero_scr[...] = jnp.zeros_like(zero_scr)
        for e in range(N_EXPERTS):
            @pl.when(lth_ref[e] == 1)
            def _():
                st = pl.multiple_of(lts_ref[e], TM)
                pltpu.make_async_copy(zero_scr, xs_ref.at[pl.ds(st, TM)], semz).start()
        for e in range(N_EXPERTS):
            @pl.when(lth_ref[e] == 1)
            def _():
                pltpu.make_async_copy(zero_scr, xs_ref.at[pl.ds(0, TM)], semz).wait()

    base = i * (TM * TOP_K)

    def issue(t, carry):
        for k in range(TOP_K):
            p = pos_ref[base + t * TOP_K + k]
            pltpu.make_async_copy(tok_ref.at[pl.ds(t, 1)], xs_ref.at[pl.ds(p, 1)], sem).start()
        return carry

    lax.fori_loop(0, TM, issue, 0)

    def drain(t, carry):
        for k in range(TOP_K):
            pltpu.make_async_copy(tok_ref.at[pl.ds(0, 1)], xs_ref.at[pl.ds(0, 1)], sem).wait()
        return carry

    lax.fori_loop(0, TM, drain, 0)


def _dispatch(pos_flat, lt_start, lt_has, tok, n_sorted):
    n, d = tok.shape
    grid_spec = pltpu.PrefetchScalarGridSpec(
        num_scalar_prefetch=3,
        grid=(n // TM,),
        in_specs=[pl.BlockSpec((TM, d), lambda i, *_: (i, 0))],
        out_specs=pl.BlockSpec(memory_space=pl.ANY),
        scratch_shapes=[pltpu.VMEM((TM, d), F32), pltpu.SemaphoreType.DMA, pltpu.SemaphoreType.DMA],
    )
    return pl.pallas_call(
        _dispatch_kernel,
        grid_spec=grid_spec,
        out_shape=jax.ShapeDtypeStruct((n_sorted, d), F32),
        compiler_params=_cparams(("arbitrary",)),
        name="moe_dispatch",
    )(pos_flat, lt_start, lt_has, tok)


def _experts_kernel(te_ref, ts_ref, nt_ref, x_ref, wg_ref, bg_ref, wu_ref, bu_ref, wd_ref, bd_ref, y_ref):
    i = pl.program_id(0)

    @pl.when(i < nt_ref[0])
    def _():
        x = x_ref[...].astype(BF16)
        g = jnp.minimum(_dot(x, wg_ref[0]) + bg_ref[0], SWIGLU_LIMIT)
        u = jnp.clip(_dot(x, wu_ref[0]) + bu_ref[0], -SWIGLU_LIMIT, SWIGLU_LIMIT)
        a = (u + 1.0) * (g * _sigmoid(SWIGLU_ALPHA * g))
        y_ref[...] = _dot(a.astype(BF16), wd_ref[0]) + bd_ref[0]


def _experts(tile_expert, tile_src, ntiles, xs, wg, bg, wu, bu, wd, bd):
    n_sorted, d = xs.shape
    ff = wg.shape[2]
    xmap = lambda i, te, ts, nt: (ts[i], 0)
    wmap = lambda i, te, ts, nt: (te[i], 0, 0)
    grid_spec = pltpu.PrefetchScalarGridSpec(
        num_scalar_prefetch=3,
        grid=(n_sorted // TM,),
        in_specs=[pl.BlockSpec((TM, d), xmap),
                  pl.BlockSpec((1, d, ff), wmap), pl.BlockSpec((1, 1, ff), wmap),
                  pl.BlockSpec((1, d, ff), wmap), pl.BlockSpec((1, 1, ff), wmap),
                  pl.BlockSpec((1, ff, d), wmap), pl.BlockSpec((1, 1, d), wmap)],
        out_specs=pl.BlockSpec((TM, d), xmap),
    )
    ne = wg.shape[0]
    return pl.pallas_call(
        _experts_kernel,
        grid_spec=grid_spec,
        out_shape=jax.ShapeDtypeStruct((n_sorted, d), F32),
        compiler_params=_cparams(("arbitrary",)),
        name="moe_experts",
    )(tile_expert, tile_src, ntiles, xs, wg, bg.reshape(ne, 1, ff), wu, bu.reshape(ne, 1, ff),
      wd, bd.reshape(ne, 1, d))


def _combine_kernel(pos_ref, ys_ref, gate_ref, x_ref, mod_ref, g_ref, o_ref, buf, sem, *, tile_map):
    base = tile_map(pl.program_id(0)) * (TM * TOP_K)

    def issue(t, carry):
        for k in range(TOP_K):
            p = pos_ref[base + t * TOP_K + k]
            pltpu.make_async_copy(ys_ref.at[pl.ds(p, 1)], buf.at[k, pl.ds(t, 1)], sem).start()
        return carry

    lax.fori_loop(0, TM, issue, 0)

    def drain(t, carry):
        for k in range(TOP_K):
            pltpu.make_async_copy(ys_ref.at[pl.ds(0, 1)], buf.at[0, pl.ds(0, 1)], sem).wait()
        return carry

    lax.fori_loop(0, TM, drain, 0)
    gates = gate_ref[...]
    f = gates[:, 0:1] * buf[0]
    for k in range(1, TOP_K):
        f = f + gates[:, k:k + 1] * buf[k]
    o_ref[...] = x_ref[...] + mod_ref[0, 5:6, :] * _rms(f, g_ref[3:4, :])


def _combine(pos_flat, ys, gates, x_new, mod_tab, g, tile_map, mod_map, n_out):
    d = x_new.shape[1]
    nt = n_out // TM
    grid_spec = pltpu.PrefetchScalarGridSpec(
        num_scalar_prefetch=1,
        grid=(nt,),
        in_specs=[pl.BlockSpec(memory_space=pl.ANY),
                  pl.BlockSpec((TM, LANES), lambda i, p: (tile_map(i), 0)),
                  pl.BlockSpec((TM, d), lambda i, p: (tile_map(i), 0)),
                  pl.BlockSpec((1, 6, d), lambda i, p: mod_map(tile_map(i))),
                  pl.BlockSpec((4, d), lambda i, p: (0, 0))],
        out_specs=pl.BlockSpec((TM, d), lambda i, p: (i, 0)),
        scratch_shapes=[pltpu.VMEM((TOP_K, TM, d), F32), pltpu.SemaphoreType.DMA],
    )
    return pl.pallas_call(
        functools.partial(_combine_kernel, tile_map=tile_map),
        grid_spec=grid_spec,
        out_shape=jax.ShapeDtypeStruct((n_out, d), F32),
        compiler_params=_cparams(("arbitrary",)),
        name="moe_combine",
    )(pos_flat, ys, gates, x_new, mod_tab, g)


def _prep_w_in(w_in):
    gw = GROUP_W
    hd = HEAD_DIM
    o = 0
    aq, ak, av = w_in[:, 0:gw], w_in[:, gw:2 * gw], w_in[:, 2 * gw:3 * gw]
    o = 3 * gw
    bq = w_in[:, o:o + gw]
    o += gw
    bk = w_in[:, o:o + 2 * hd]
    o += 2 * hd
    bv = w_in[:, o:o + 2 * hd]
    o += 2 * hd
    cz = w_in[:, o:o + gw]
    o += gw
    cxbc = w_in[:, o:o + 2 * gw]
    o += 2 * gw
    cdt = w_in[:, o:o + SSM_HEADS]
    o += SSM_HEADS
    duv = w_in[:, o:o + 2 * gw]
    scale = HEAD_DIM ** -0.5
    dup = lambda t: jnp.concatenate([t[:, 0:hd], t[:, 0:hd], t[:, hd:2 * hd], t[:, hd:2 * hd]], axis=1)
    cdt_p = jnp.pad(cdt, ((0, 0), (0, LANES - SSM_HEADS)))
    return jnp.concatenate([aq * scale, ak, av, bq * scale, dup(bk), dup(bv), cxbc, cz, cdt_p, duv],
                           axis=1).astype(BF16)


def _rope_tables(s, lc):
    t = np.arange(s)
    qd = HEAD_DIM // 4
    inv = ROPE_BASE ** (-np.arange(qd, dtype=np.float32) / qd)
    ang_r = (t // GRID_W).astype(np.float32)[:, None] * inv[None, :]
    ang_c = (t % GRID_W).astype(np.float32)[:, None] * inv[None, :]
    ang = np.concatenate([ang_r, ang_r, ang_c, ang_c], axis=1)
    sign = np.concatenate([-np.ones(qd), np.ones(qd), -np.ones(qd), np.ones(qd)]).astype(np.float32)
    cos = np.concatenate([np.ones((lc, HEAD_DIM), np.float32), np.cos(ang)], axis=0)
    sin = np.concatenate([np.zeros((lc, HEAD_DIM), np.float32), np.sin(ang) * sign[None, :]], axis=0)
    return jnp.asarray(np.tile(cos, (1, 2)), F32), jnp.asarray(np.tile(sin, (1, 2)), F32)


def _na_bias_indices(rows):
    nblk = rows // NA_QROWS
    nq = NA_QROWS * GRID_W
    nk = NA_KROWS * GRID_W
    dr = np.zeros((3, nq, nk), np.int32)
    dc = np.zeros((3, nq, nk), np.int32)
    ok = np.zeros((3, nq, nk), bool)
    for var, g in enumerate((0, 1, nblk - 1)):
        sr = int(np.clip(NA_QROWS * g - NA_WIN_ROWS // 2, 0, rows - NA_KROWS))
        for ri in range(NA_QROWS):
            r = NA_QROWS * g + ri
            kr0 = int(np.clip(r - NA_WIN_ROWS // 2, 0, rows - NA_WIN_ROWS))
            qc = np.arange(GRID_W)[:, None]
            kc = np.arange(GRID_W)[None, :]
            ws = np.clip(qc - NA_WIN_COLS // 2, 0, GRID_W - NA_WIN_COLS)
            cok = (kc >= ws) & (kc < ws + NA_WIN_COLS)
            dcc = np.clip(kc - qc + NA_WIN_COLS - 1, 0, 2 * NA_WIN_COLS - 2)
            for jj in range(NA_KROWS):
                ar = sr + jj
                rok = kr0 <= ar < kr0 + NA_WIN_ROWS
                sl = (var, slice(ri * GRID_W, (ri + 1) * GRID_W), slice(jj * GRID_W, (jj + 1) * GRID_W))
                dc[sl] = dcc
                if rok:
                    dr[sl] = ar - r + NA_WIN_ROWS - 1
                    ok[sl] = cok
    return dr, dc, ok


def _na_bias(rpb, rows):
    dr, dc, ok = _na_bias_indices(rows)
    b = rpb.astype(F32)[:, dr, dc]
    b = jnp.where(ok[None], b, NEG_INF)
    return b.transpose(1, 0, 2, 3)


def _layer(x_all, cvec, nb, sa, lc, last, w_mod, b_mod, norm_g, w_in, w_out, na_rpb, wa_sink,
           ssm_conv_w, ssm_conv_b, ssm_a_log, ssm_dt_bias, ssm_d, ssm_norm_g, gm_ln_g, gm_ln_b, gm_ws, gm_bs,
           router_w, router_b, exp_w_gate, exp_b_gate, exp_w_up, exp_b_up, exp_w_down, exp_b_down,
           cos_tab, sin_tab):
    n, d = x_all.shape
    tpb = sa // TM
    mod_tab = _modulation(cvec, w_mod, b_mod).reshape(cvec.shape[0], 6, d)

    amat, bmat, cmat, dmat = _inproj(x_all, mod_tab, norm_g[0], _prep_w_in(w_in), cos_tab, sin_tab, tpb, nb)

    rows = (sa - lc) // GRID_W
    o_a = _na_attention(amat.reshape(nb, sa, COLS_A), _na_bias(na_rpb, rows), nb, sa, lc).reshape(n, GROUP_W)
    o_b = _wa_attention(bmat.reshape(nb, sa, COLS_B), wa_sink.astype(F32), nb, sa, lc).reshape(n, GROUP_W)
    o_c = _ssd(cmat.reshape(nb, sa, COLS_C), ssm_conv_w, ssm_conv_b, ssm_a_log, ssm_dt_bias, ssm_d, ssm_norm_g,
               nb, sa, lc).reshape(n, GROUP_W)
    o_d = _gmlp(dmat, gm_ln_g, gm_ln_b, gm_ws, gm_bs)

    rw_p = jnp.pad(router_w, ((0, 0), (0, LANES - N_EXPERTS))).astype(BF16)
    rb_p = jnp.concatenate([router_b.astype(F32), jnp.full((LANES - N_EXPERTS,), NEG_INF, F32)]).reshape(1, LANES)
    x_new, tok, sel, idx4, gates = _outproj(o_a, o_b, o_c, o_d, x_all, mod_tab, norm_g, w_out.astype(BF16),
                                            rw_p, rb_p, tpb, nb)

    pos, meta = _route(sel, idx4)
    offs = meta[0, :N_EXPERTS]
    padded = meta[1, :N_EXPERTS]
    ends = offs + padded
    nt_max = (n * TOP_K) // TM + N_EXPERTS
    tile_start = jnp.arange(nt_max, dtype=I32) * TM
    ntiles = (ends[N_EXPERTS - 1] // TM).astype(I32)
    tile_src = jnp.minimum(jnp.arange(nt_max, dtype=I32), ntiles - 1)
    tile_expert = jnp.sum((ends[None, :] <= (tile_src * TM)[:, None]).astype(I32), axis=1).astype(I32)
    tile_expert = jnp.minimum(tile_expert, N_EXPERTS - 1)
    lt_start = jnp.maximum(ends - TM, 0).astype(I32)
    lt_has = (padded > 0).astype(I32)
    pos_flat = pos[:, :TOP_K].reshape(-1)

    xs = _dispatch(pos_flat, lt_start, lt_has, tok, nt_max * TM)
    ys = _experts(tile_expert, tile_src, ntiles.reshape(1), xs,
                  exp_w_gate.astype(BF16), exp_b_gate, exp_w_up.astype(BF16), exp_b_up,
                  exp_w_down.astype(BF16), exp_b_down)

    def mod_map(i):
        return (jnp.where(i % tpb == 0, nb, i // tpb), 0, 0)

    if last:
        lt = tpb - 1
        tile_map = lambda i: (i // lt) * tpb + 1 + i % lt
        n_out = nb * lt * TM
    else:
        tile_map = lambda i: i
        n_out = n
    return _combine(pos_flat, ys, gates, x_new, mod_tab, norm_g, tile_map, mod_map, n_out)


def kernel(x, c, ctx, c_ctx, w_mod, b_mod, norm_g, w_in, w_out, na_rpb, wa_sink, ssm_conv_w, ssm_conv_b, ssm_a_log,
           ssm_dt_bias, ssm_d, ssm_norm_g, gm_ln_g, gm_ln_b, gm_ws, gm_bs, router_w, router_b, exp_w_gate,
           exp_b_gate, exp_w_up, exp_b_up, exp_w_down, exp_b_down):
    nb, s, d = x.shape
    lc = ctx.shape[1]
    sa = lc + s
    depth = w_mod.shape[0]
    assert lc == TM and s % TM == 0 and s // GRID_W >= NA_KROWS + 1
    x_all = jnp.concatenate([ctx, x], axis=1).reshape(nb * sa, d)
    cvec = jnp.concatenate([c, c_ctx[None, :], jnp.zeros((8 - nb - 1, d), F32)], axis=0)
    cos_tab, sin_tab = _rope_tables(s, lc)
    for l in range(depth):
        x_all = _layer(x_all, cvec, nb, sa, lc, l == depth - 1,
                       w_mod[l], b_mod[l], norm_g[l], w_in[l], w_out[l], na_rpb[l], wa_sink[l],
                       ssm_conv_w[l], ssm_conv_b[l], ssm_a_log[l], ssm_dt_bias[l], ssm_d[l], ssm_norm_g[l],
                       gm_ln_g[l], gm_ln_b[l], gm_ws[l], gm_bs[l], router_w[l], router_b[l],
                       exp_w_gate[l], exp_b_gate[l], exp_w_up[l], exp_b_up[l], exp_w_down[l], exp_b_down[l],
                       cos_tab, sin_tab)
    return x_all.reshape(nb, s, d)
```
